```python
import jax, jax.numpy as jnp
from jax import lax
import numpy as np

D_MODEL = 1024
BATCH = 32
SEQ = 2048
DEPTH = 4

PLE_DIM = 256
HEAD_DIM = 64
D_CONF = 3 * D_MODEL // 8
D_SHORT = 3 * D_MODEL // 8
D_POOL = D_MODEL - D_CONF - D_SHORT
D_IN = 2 * D_CONF + 3 * D_SHORT + D_POOL
IN_SPLITS = tuple(int(v) for v in np.cumsum([D_CONF, D_CONF, D_SHORT, D_SHORT, D_SHORT]))
CONF_KERNEL = 31
SHORT_KERNEL = 3
POOL_WINDOWS = (2, 4, 8, 16)
N_POOL_GROUPS = len(POOL_WINDOWS)
POOL_GROUP = D_POOL // N_POOL_GROUPS
N_EXPERTS = 32
TOP_K = 4
D_EXPERT = D_MODEL
SWIGLU_LIMIT = 7.0
SWIGLU_ALPHA = 1.702
MOE_BLOCK = 128
LN_EPS = 1e-5
DEEPNORM_ALPHA = float((2 * DEPTH) ** 0.25)
DEEPNORM_BETA = float((8 * DEPTH) ** -0.25)

kernel_name = "hybrid_conv_pool_moe_deepnorm"


def layer_norm(x, g, b):
    xf = x.astype(jnp.float32)
    mu = xf.mean(-1, keepdims=True)
    var = jnp.square(xf - mu).mean(-1, keepdims=True)
    y = (xf - mu) * lax.rsqrt(var + LN_EPS)
    return (y * g.astype(jnp.float32) + b.astype(jnp.float32)).astype(x.dtype)


def causal_depthwise_conv(u, w):
    k, c = w.shape
    return lax.conv_general_dilated(
        u, w[:, None, :].astype(u.dtype), window_strides=(1,), padding=[(k - 1, 0)],
        dimension_numbers=("NWC", "WIO", "NWC"), feature_group_count=c)


def multiscale_pool(v, pool_w, pool_scale):
    b, s, _ = v.shape
    vg = v.astype(jnp.float32).reshape(b, s, N_POOL_GROUPS, POOL_GROUP)
    cs = jnp.cumsum(vg, axis=1)
    means = []
    for g, w in enumerate(POOL_WINDOWS):
        c = cs[:, :, g]
        prev = jnp.pad(c[:, : s - w], ((0, 0), (w, 0), (0, 0)))
        cnt = jnp.minimum(jnp.arange(1, s + 1), w).astype(jnp.float32)[None, :, None]
        means.append((c - prev) / cnt)
    pooled = (jnp.stack(means, axis=2) - vg).astype(v.dtype)
    mixed = jnp.einsum("bsgc,gcd->bsgd", pooled, pool_w)
    return mixed.reshape(b, s, D_POOL) * pool_scale


def hybrid_mixer(x, w_in, conv_a_w, conv_a_b, ln_a_g, ln_a_b, conv_b_w, pool_w, pool_scale, w_out):
    z = x @ w_in
    a_val, a_gate, b_gate, c_gate, b_val, pool_in = jnp.split(z, IN_SPLITS, axis=-1)
    u = a_val * jax.nn.sigmoid(a_gate)
    u = causal_depthwise_conv(u, conv_a_w) + conv_a_b
    y_a = jax.nn.silu(layer_norm(u, ln_a_g, ln_a_b))
    y_b = b_gate * causal_depthwise_conv(c_gate * b_val, conv_b_w)
    y_c = multiscale_pool(pool_in, pool_w, pool_scale)
    return jnp.concatenate([y_a, y_b, y_c], axis=-1) @ w_out


def moe_ffn(h, router_w, router_b, w_gate_up, b_gate_up, w_down, b_down):
    bsz, s, d = h.shape
    n = bsz * s
    hf = h.reshape(n, d)
    logits = (hf @ router_w + router_b).astype(jnp.float32)
    top_logits, top_idx = lax.top_k(logits, TOP_K)
    gates = jax.nn.softmax(top_logits, axis=-1)
    e_flat = top_idx.reshape(-1)
    order = jnp.argsort(e_flat)
    e_sorted = e_flat[order]
    tok_sorted = order // TOP_K
    gate_sorted = gates.reshape(-1)[order].astype(h.dtype)
    counts = jnp.bincount(e_flat, length=N_EXPERTS)
    starts = jnp.cumsum(counts) - counts
    padded = (counts + MOE_BLOCK - 1) // MOE_BLOCK * MOE_BLOCK
    pad_ends = jnp.cumsum(padded)
    pad_starts = pad_ends - padded
    dest = pad_starts[e_sorted] + (jnp.arange(n * TOP_K) - starts[e_sorted])
    n_blocks = -(-(n * TOP_K) // MOE_BLOCK) + N_EXPERTS
    n_rows = n_blocks * MOE_BLOCK
    buf = jnp.zeros((n_rows, d), h.dtype).at[dest].set(hf[tok_sorted])
    block_start = jnp.arange(n_blocks) * MOE_BLOCK
    block_expert = jnp.minimum(jnp.searchsorted(pad_ends, block_start, side="right"), N_EXPERTS - 1)

    def expert_block(args):
        xb, e = args
        gu = xb @ w_gate_up[e] + b_gate_up[e]
        gate, up = jnp.split(gu, 2, axis=-1)
        gate = jnp.minimum(gate, SWIGLU_LIMIT)
        up = jnp.clip(up, -SWIGLU_LIMIT, SWIGLU_LIMIT)
        act = (up + 1.0) * gate * jax.nn.sigmoid(SWIGLU_ALPHA * gate)
        return act @ w_down[e] + b_down[e]

    yb = lax.map(expert_block, (buf.reshape(n_blocks, MOE_BLOCK, d), block_expert))
    y_sorted = yb.reshape(n_rows, d)[dest] * gate_sorted[:, None]
    out = jnp.zeros((n, d), h.dtype).at[tok_sorted].add(y_sorted)
    return out.reshape(bsz, s, d)


def setup_inputs(seed: int = 0) -> dict:
    key = jax.random.key(seed)
    ks = jax.random.split(key, 23)
    f32 = jnp.float32
    nrm = lambda k, shape, scale: jax.random.normal(k, shape, f32) * scale
    L = DEPTH
    return {
        "x": nrm(ks[0], (BATCH, SEQ, D_MODEL), 1.0),
        "p": nrm(ks[1], (DEPTH, BATCH, SEQ, PLE_DIM), 1.0),
        "w_in": nrm(ks[2], (L, D_MODEL, D_IN), D_MODEL ** -0.5),
        "conv_a_w": nrm(ks[3], (L, CONF_KERNEL, D_CONF), CONF_KERNEL ** -0.5),
        "conv_a_b": nrm(ks[4], (L, D_CONF), 0.02),
        "ln_a_g": 1.0 + nrm(ks[5], (L, D_CONF), 0.05),
        "ln_a_b": nrm(ks[6], (L, D_CONF), 0.02),
        "conv_b_w": nrm(ks[7], (L, SHORT_KERNEL, D_SHORT), SHORT_KERNEL ** -0.5),
        "pool_w": nrm(ks[8], (L, N_POOL_GROUPS, POOL_GROUP, POOL_GROUP), POOL_GROUP ** -0.5),
        "pool_scale": 1.0 + nrm(ks[9], (L, D_POOL), 0.1),
        "w_out": nrm(ks[10], (L, D_MODEL, D_MODEL), DEEPNORM_BETA * D_MODEL ** -0.5),
        "ln1_g": 1.0 + nrm(ks[11], (L, D_MODEL), 0.05),
        "ln1_b": nrm(ks[12], (L, D_MODEL), 0.02),
        "router_w": nrm(ks[13], (L, D_MODEL, N_EXPERTS), D_MODEL ** -0.5),
        "router_b": nrm(ks[14], (L, N_EXPERTS), 0.01),
        "w_gate_up": nrm(ks[15], (L, N_EXPERTS, D_MODEL, 2 * D_EXPERT), D_MODEL ** -0.5),
        "b_gate_up": nrm(ks[16], (L, N_EXPERTS, 2 * D_EXPERT), 0.02),
        "w_down": nrm(ks[17], (L, N_EXPERTS, D_EXPERT, D_MODEL), DEEPNORM_BETA * D_EXPERT ** -0.5),
        "b_down": nrm(ks[18], (L, N_EXPERTS, D_MODEL), 0.02),
        "ple_w_gate": nrm(ks[19], (L, D_MODEL, D_MODEL), D_MODEL ** -0.5),
        "ple_w_proj": nrm(ks[20], (L, PLE_DIM, D_MODEL), DEEPNORM_BETA * PLE_DIM ** -0.5),
        "ln2_g": 1.0 + nrm(ks[21], (L, D_MODEL), 0.05),
        "ln2_b": nrm(ks[22], (L, D_MODEL), 0.02),
    }


def reference(x, p, w_in, conv_a_w, conv_a_b, ln_a_g, ln_a_b, conv_b_w, pool_w, pool_scale, w_out,
              ln1_g, ln1_b, router_w, router_b, w_gate_up, b_gate_up, w_down, b_down,
              ple_w_gate, ple_w_proj, ln2_g, ln2_b):
    for i in range(DEPTH):
        mix = hybrid_mixer(x, w_in[i], conv_a_w[i], conv_a_b[i], ln_a_g[i], ln_a_b[i], conv_b_w[i],
                           pool_w[i], pool_scale[i], w_out[i])
        h = layer_norm(DEEPNORM_ALPHA * x + mix, ln1_g[i], ln1_b[i])
        ffn = moe_ffn(h, router_w[i], router_b[i], w_gate_up[i], b_gate_up[i], w_down[i], b_down[i])
        ple = jax.nn.sigmoid(h @ ple_w_gate[i]) * (p[i] @ ple_w_proj[i])
        x = layer_norm(DEEPNORM_ALPHA * h + ffn + ple, ln2_g[i], ln2_b[i])
    return x
```

```python
import functools

import jax
import jax.numpy as jnp
import numpy as np
from jax import lax
from jax.experimental import pallas as pl
from jax.experimental.pallas import tpu as pltpu

HEAD_DIM = 64
CONF_KERNEL = 31
SHORT_KERNEL = 3
POOL_WINDOWS = (2, 4, 8, 16)
N_EXPERTS = 32
TOP_K = 4
SWIGLU_LIMIT = 7.0
SWIGLU_ALPHA = 1.702
LN_EPS = 1e-5

SUBLANES = 8
LANES = 128
VMEM_LIMIT_BYTES = 56 * 1024 * 1024

SEQ_TILE = 512
ROW_BLOCK = 512
CONV_CHUNK = 32
HALO = 32
FFN_CHUNK = 512

_BF16 = jnp.bfloat16
_F32 = jnp.float32


def _layer_norm(v, g, b):
    mu = jnp.mean(v, axis=-1, keepdims=True)
    c = v - mu
    var = jnp.mean(c * c, axis=-1, keepdims=True)
    return c * lax.rsqrt(var + LN_EPS) * g + b


def _dot(a, b):
    return jnp.dot(a, b, preferred_element_type=_F32)


def _dot_nt(a, b):
    return lax.dot_general(a, b, (((1,), (1,)), ((), ())), preferred_element_type=_F32)


def _mixer_kernel(alpha, d_conf, d_pool,
                  x_ref, w_in_ref, caw_ref, cab_ref, lag_ref, lab_ref, cbw_ref, pbd_ref, psc_ref,
                  w_out_ref, l1g_ref, l1b_ref, rwh_ref, rwl_ref, rb_ref,
                  h_ref, idx_ref, gate_ref, pos_ref, cnt_ref,
                  ush_ref, cbx_ref, vx_ref, s2_ref, s4_ref, s8_ref, ycat_ref, tri_ref):
    ts = x_ref.shape[0]
    b = pl.program_id(0)
    s = pl.program_id(1)
    first = jnp.logical_and(b == 0, s == 0)

    @pl.when(first)
    def _():
        cnt_ref[...] = jnp.zeros_like(cnt_ref)
        r = lax.broadcasted_iota(jnp.int32, (ts, ts), 0)
        c = lax.broadcasted_iota(jnp.int32, (ts, ts), 1)
        tri_ref[...] = (r < c).astype(_BF16)

    @pl.when(s == 0)
    def _():
        ush_ref[0, 0:HALO, :] = jnp.zeros((HALO, d_conf), _F32)
        cbx_ref[0:SUBLANES, :] = jnp.zeros((SUBLANES, d_conf), _F32)
        vx_ref[0:HALO, :] = jnp.zeros((HALO, d_pool), _F32)

    x = x_ref[...]
    xb = x.astype(_BF16)
    c1, c2, c3, c4, c5 = (d_conf * i for i in range(1, 6))

    za = _dot(xb, w_in_ref[:, 0:c2])
    u = za[:, 0:c1] * jax.nn.sigmoid(za[:, c1:c2])
    ush_ref[0, HALO:HALO + ts, :] = u
    for r in range(1, SUBLANES):
        ush_ref[r, 0:ts + HALO - SUBLANES, :] = ush_ref[0, r:r + ts + HALO - SUBLANES, :]

    def conv_chunk(ci, carry):
        base = pl.multiple_of(ci * CONV_CHUNK, CONV_CHUNK)
        acc = jnp.zeros((CONV_CHUNK, d_conf), _F32)
        for j in range(CONF_KERNEL):
            q, r = divmod(HALO - (CONF_KERNEL - 1) + j, SUBLANES)
            acc = acc + caw_ref[j:j + 1, :] * ush_ref[r, pl.ds(base + q * SUBLANES, CONV_CHUNK), :]
        acc = acc + cab_ref[...]
        ya = _layer_norm(acc, lag_ref[...], lab_ref[...])
        ya = ya * jax.nn.sigmoid(ya)
        ycat_ref[pl.ds(base, CONV_CHUNK), 0:d_conf] = ya.astype(_BF16)
        return carry

    lax.fori_loop(0, ts // CONV_CHUNK, conv_chunk, 0)
    ush_ref[0, 0:HALO, :] = ush_ref[0, ts:ts + HALO, :]

    zb = _dot(xb, w_in_ref[:, c2:c5])
    cb = zb[:, c1:c2] * zb[:, c2:c3]
    cbx_ref[SUBLANES:SUBLANES + ts, :] = cb
    conv_b = cbw_ref[SHORT_KERNEL - 1:SHORT_KERNEL, :] * cb
    for j in range(SHORT_KERNEL - 1):
        off = SUBLANES - (SHORT_KERNEL - 1) + j
        conv_b = conv_b + cbw_ref[j:j + 1, :] * cbx_ref[off:off + ts, :]
    ycat_ref[:, c1:c2] = (zb[:, 0:c1] * conv_b).astype(_BF16)
    cbx_ref[0:SUBLANES, :] = cbx_ref[ts:ts + SUBLANES, :]

    v = _dot(xb, w_in_ref[:, c5:c5 + d_pool])
    vx_ref[HALO:HALO + ts, :] = v
    n8 = ts + HALO - 8
    s2_ref[8:8 + n8, :] = vx_ref[8:8 + n8, :] + vx_ref[7:7 + n8, :]
    n16 = ts + HALO - 16
    s4_ref[16:16 + n16, :] = s2_ref[16:16 + n16, :] + s2_ref[14:14 + n16, :]
    n24 = ts + HALO - 24
    s8_ref[24:24 + n24, :] = s4_ref[24:24 + n24, :] + s4_ref[20:20 + n24, :]
    s16 = s8_ref[HALO:HALO + ts, :] + s8_ref[HALO - 8:HALO - 8 + ts, :]
    sums = (s2_ref[HALO:HALO + ts, :], s4_ref[HALO:HALO + ts, :], s8_ref[HALO:HALO + ts, :], s16)
    tpos = s * ts + lax.broadcasted_iota(jnp.int32, (ts, 1), 0) + 1
    lane = lax.broadcasted_iota(jnp.int32, (ts, d_pool), 1)
    group = d_pool // len(POOL_WINDOWS)
    mean = None
    for gi, w in enumerate(POOL_WINDOWS):
        m = sums[gi] / jnp.minimum(tpos, w).astype(_F32)
        mean = m if mean is None else jnp.where(lane >= gi * group, m, mean)
    pooled = (mean - v).astype(_BF16)
    ycat_ref[:, c2:c2 + d_pool] = (_dot(pooled, pbd_ref[...]) * psc_ref[...]).astype(_BF16)
    vx_ref[0:HALO, :] = vx_ref[ts:ts + HALO, :]

    mix = _dot(ycat_ref[...], w_out_ref[...])
    h = _layer_norm(alpha * x + mix, l1g_ref[...], l1b_ref[...])
    h_ref[...] = h

    hh = h.astype(_BF16)
    hl = (h - hh.astype(_F32)).astype(_BF16)
    logits = _dot_nt(rwh_ref[...], hh) + _dot_nt(rwh_ref[...], hl) + _dot_nt(rwl_ref[...], hh) + rb_ref[...]
    ne = logits.shape[0]
    eid = lax.broadcasted_iota(jnp.int32, (ne, ts), 0)
    work = logits
    sels, tops = [], []
    for k in range(TOP_K):
        m = jnp.max(work, axis=0, keepdims=True)
        ik = jnp.min(jnp.where(work == m, eid, ne), axis=0, keepdims=True)
        sel = eid == ik
        work = jnp.where(sel, -jnp.inf, work)
        idx_ref[k:k + 1, :] = ik
        sels.append(sel)
        tops.append(m)
    exps = [jnp.exp(t - tops[0]) for t in tops]
    denom = exps[0] + exps[1] + exps[2] + exps[3]
    for k in range(TOP_K):
        gate_ref[k:k + 1, :] = exps[k] / denom

    chosen = jnp.logical_or(jnp.logical_or(sels[0], sels[1]), jnp.logical_or(sels[2], sels[3]))
    chosen_f = chosen.astype(_F32)
    before = _dot(chosen_f.astype(_BF16), tri_ref[...]) + cnt_ref[:, 0:1]
    for k in range(TOP_K):
        pk = jnp.sum(jnp.where(sels[k], before, 0.0), axis=0, keepdims=True)
        pos_ref[k:k + 1, :] = pk.astype(jnp.int32)
    cnt_ref[...] = cnt_ref[...] + jnp.sum(chosen_f, axis=1, keepdims=True)


def _mixer(x, lw, alpha):
    bsz, seq, d = x.shape
    ts = SEQ_TILE
    n_st = seq // ts
    d_conf = lw["conv_a_w"].shape[1]
    d_pool = lw["pool_scale"].shape[1]
    d_in = lw["w_in"].shape[1]
    const = lambda *shape: pl.BlockSpec(shape, lambda b, s: (0,) * len(shape))
    tile_idx = lambda b, s: (b * n_st + s, 0, 0)
    kern = functools.partial(_mixer_kernel, alpha, d_conf, d_pool)
    return pl.pallas_call(
        kern,
        grid=(bsz, n_st),
        in_specs=[
            pl.BlockSpec((None, ts, d), lambda b, s: (b, s, 0)),
            const(d, d_in), const(CONF_KERNEL, d_conf), const(1, d_conf), const(1, d_conf), const(1, d_conf),
            const(SHORT_KERNEL, d_conf), const(d_pool, d_pool), const(1, d_pool),
            const(d, d), const(1, d), const(1, d),
            const(N_EXPERTS, d), const(N_EXPERTS, d), const(N_EXPERTS, 1),
        ],
        out_specs=[
            pl.BlockSpec((None, ts, d), lambda b, s: (b, s, 0)),
            pl.BlockSpec((None, TOP_K, ts), tile_idx),
            pl.BlockSpec((None, TOP_K, ts), tile_idx),
            pl.BlockSpec((None, TOP_K, ts), tile_idx),
            pl.BlockSpec((N_EXPERTS, LANES), lambda b, s: (0, 0)),
        ],
        out_shape=[
            jax.ShapeDtypeStruct((bsz, seq, d), _F32),
            jax.ShapeDtypeStruct((bsz * n_st, TOP_K, ts), jnp.int32),
            jax.ShapeDtypeStruct((bsz * n_st, TOP_K, ts), _F32),
            jax.ShapeDtypeStruct((bsz * n_st, TOP_K, ts), jnp.int32),
            jax.ShapeDtypeStruct((N_EXPERTS, LANES), _F32),
        ],
        scratch_shapes=[
            pltpu.VMEM((SUBLANES, ts + HALO, d_conf), _F32),
            pltpu.VMEM((ts + SUBLANES, d_conf), _F32),
            pltpu.VMEM((ts + HALO, d_pool), _F32),
            pltpu.VMEM((ts + HALO, d_pool), _F32),
            pltpu.VMEM((ts + HALO, d_pool), _F32),
            pltpu.VMEM((ts + HALO, d_pool), _F32),
            pltpu.VMEM((ts, d), _BF16),
            pltpu.VMEM((ts, ts), _BF16),
        ],
        compiler_params=pltpu.CompilerParams(
            dimension_semantics=("arbitrary", "arbitrary"), vmem_limit_bytes=VMEM_LIMIT_BYTES),
        name="mixer",
    )(x, lw["w_in"], lw["conv_a_w"], lw["conv_a_b"], lw["ln_a_g"], lw["ln_a_b"], lw["conv_b_w"],
      lw["pool_bd"], lw["pool_scale"], lw["w_out"], lw["ln1_g"], lw["ln1_b"],
      lw["router_hi"], lw["router_lo"], lw["router_b"])


def _row_copy_in(src_ref, dst_ref, sem, src_row, dst_row):
    return pltpu.make_async_copy(src_ref.at[pl.ds(src_row, 1), :], dst_ref.at[pl.ds(dst_row, 1), :], sem)


def _dispatch_kernel(start_ref, idx_ref, pos_ref, h_ref, buf_in_ref, buf_ref, sem):
    del buf_in_ref
    ts = h_ref.shape[0]
    n = TOP_K * ts

    def copy(i):
        dst = start_ref[idx_ref[0, i]] + pos_ref[0, i]
        return _row_copy_in(h_ref, buf_ref, sem, i % ts, dst)

    def start(i, c):
        copy(i).start()
        return c

    def wait(i, c):
        copy(i).wait()
        return c

    lax.fori_loop(0, n, start, 0)
    lax.fori_loop(0, n, wait, 0)


def _dispatch(h2d, idx_s, pos_s, starts, n_rows):
    n, d = h2d.shape
    ts = SEQ_TILE
    smem_tile = pl.BlockSpec((None, 1, TOP_K * ts), lambda t, st: (t, 0, 0), memory_space=pltpu.SMEM)
    zeros = jnp.zeros((n_rows, d), _F32)
    return pl.pallas_call(
        _dispatch_kernel,
        grid_spec=pltpu.PrefetchScalarGridSpec(
            num_scalar_prefetch=1,
            grid=(n // ts,),
            in_specs=[smem_tile, smem_tile,
                      pl.BlockSpec((ts, d), lambda t, st: (t, 0)),
                      pl.BlockSpec(memory_space=pl.ANY)],
            out_specs=pl.BlockSpec(memory_space=pl.ANY),
            scratch_shapes=[pltpu.SemaphoreType.DMA],
        ),
        out_shape=jax.ShapeDtypeStruct((n_rows, d), _F32),
        input_output_aliases={4: 0},
        compiler_params=pltpu.CompilerParams(
            dimension_semantics=("arbitrary",), vmem_limit_bytes=VMEM_LIMIT_BYTES),
        name="dispatch",
    )(starts, idx_s, pos_s, h2d, zeros)


def _expert_kernel(be_ref, x_ref, wgu_ref, bgu_ref, wd_ref, bd_ref, y_ref):
    del be_ref
    d_e = wd_ref.shape[0]
    xb = x_ref[...].astype(_BF16)
    acc = jnp.zeros(y_ref.shape, _F32)
    for j in range(d_e // FFN_CHUNK):
        lo, hi = j * FFN_CHUNK, (j + 1) * FFN_CHUNK
        gate = _dot(xb, wgu_ref[:, lo:hi]) + bgu_ref[:, lo:hi]
        up = _dot(xb, wgu_ref[:, d_e + lo:d_e + hi]) + bgu_ref[:, d_e + lo:d_e + hi]
        gate = jnp.minimum(gate, SWIGLU_LIMIT)
        up = jnp.clip(up, -SWIGLU_LIMIT, SWIGLU_LIMIT)
        act = (up + 1.0) * gate * jax.nn.sigmoid(SWIGLU_ALPHA * gate)
        acc = acc + _dot(act.astype(_BF16), wd_ref[lo:hi, :])
    y_ref[...] = acc + bd_ref[...]


def _experts(buf, block_expert, lw):
    n_rows, d = buf.shape
    tm = ROW_BLOCK
    d_e = lw["w_down"].shape[1]
    by_expert = lambda *shape: pl.BlockSpec((None,) + shape, lambda i, be: (be[i],) + (0,) * len(shape))
    return pl.pallas_call(
        _expert_kernel,
        grid_spec=pltpu.PrefetchScalarGridSpec(
            num_scalar_prefetch=1,
            grid=(n_rows // tm,),
            in_specs=[pl.BlockSpec((tm, d), lambda i, be: (i, 0)),
                      by_expert(d, 2 * d_e), by_expert(1, 2 * d_e), by_expert(d_e, d), by_expert(1, d)],
            out_specs=pl.BlockSpec((tm, d), lambda i, be: (i, 0)),
        ),
        out_shape=jax.ShapeDtypeStruct((n_rows, d), _F32),
        compiler_params=pltpu.CompilerParams(
            dimension_semantics=("arbitrary",), vmem_limit_bytes=VMEM_LIMIT_BYTES),
        name="experts",
    )(block_expert, buf, lw["w_gate_up"], lw["b_gate_up"], lw["w_down"], lw["b_down"])


def _combine_kernel(alpha, start_ref, idx_ref, pos_ref, gate_ref, h_ref, p_ref, y_ref,
                    pgw_ref, ppw_ref, l2g_ref, l2b_ref, out_ref, rows_ref, sem):
    ts = h_ref.shape[0]
    n = TOP_K * ts

    def copy(i):
        src = start_ref[idx_ref[0, i]] + pos_ref[0, i]
        return _row_copy_in(y_ref, rows_ref, sem, src, i)

    def start(i, c):
        copy(i).start()
        return c

    def wait(i, c):
        copy(i).wait()
        return c

    lax.fori_loop(0, n, start, 0)

    h = h_ref[...]
    ple = jax.nn.sigmoid(_dot(h.astype(_BF16), pgw_ref[...])) * _dot(p_ref[...].astype(_BF16), ppw_ref[...])
    gates = jnp.transpose(gate_ref[...])
    acc = alpha * h + ple

    lax.fori_loop(0, n, wait, 0)
    for k in range(TOP_K):
        acc = acc + gates[:, k:k + 1] * rows_ref[k * ts:(k + 1) * ts, :]
    out_ref[...] = _layer_norm(acc, l2g_ref[...], l2b_ref[...])


def _combine(h2d, p2d, y, idx_s, pos_s, gates_p, starts, lw, alpha):
    n, d = h2d.shape
    ts = SEQ_TILE
    dp = p2d.shape[1]
    smem_tile = pl.BlockSpec((None, 1, TOP_K * ts), lambda t, st: (t, 0, 0), memory_space=pltpu.SMEM)
    const = lambda *shape: pl.BlockSpec(shape, lambda t, st: (0,) * len(shape))
    return pl.pallas_call(
        functools.partial(_combine_kernel, alpha),
        grid_spec=pltpu.PrefetchScalarGridSpec(
            num_scalar_prefetch=1,
            grid=(n // ts,),
            in_specs=[smem_tile, smem_tile,
                      pl.BlockSpec((None, LANES, ts), lambda t, st: (t, 0, 0)),
                      pl.BlockSpec((ts, d), lambda t, st: (t, 0)),
                      pl.BlockSpec((ts, dp), lambda t, st: (t, 0)),
                      pl.BlockSpec(memory_space=pl.ANY),
                      const(d, d), const(dp, d), const(1, d), const(1, d)],
            out_specs=pl.BlockSpec((ts, d), lambda t, st: (t, 0)),
            scratch_shapes=[pltpu.VMEM((TOP_K * ts, d), _F32), pltpu.SemaphoreType.DMA],
        ),
        out_shape=jax.ShapeDtypeStruct((n, d), _F32),
        compiler_params=pltpu.CompilerParams(
            dimension_semantics=("arbitrary",), vmem_limit_bytes=VMEM_LIMIT_BYTES),
        name="combine",
    )(starts, idx_s, pos_s, gates_p, h2d, p2d, y, lw["ple_w_gate"], lw["ple_w_proj"], lw["ln2_g"], lw["ln2_b"])


def _layer_weights(i, w_in, conv_a_w, conv_a_b, ln_a_g, ln_a_b, conv_b_w, pool_w, pool_scale, w_out,
                   ln1_g, ln1_b, router_w, router_b, w_gate_up, b_gate_up, w_down, b_down,
                   ple_w_gate, ple_w_proj, ln2_g, ln2_b):
    row = lambda a: a[i][None, :]
    n_groups, group = pool_w.shape[1], pool_w.shape[2]
    d_pool = n_groups * group
    pool_bd = jnp.zeros((d_pool, d_pool), _F32)
    for g in range(n_groups):
        pool_bd = lax.dynamic_update_slice(pool_bd, pool_w[i, g], (g * group, g * group))
    router_t = router_w[i].T
    router_hi = router_t.astype(_BF16)
    router_lo = (router_t - router_hi.astype(_F32)).astype(_BF16)
    return dict(
        w_in=w_in[i].astype(_BF16), conv_a_w=conv_a_w[i], conv_a_b=row(conv_a_b), ln_a_g=row(ln_a_g),
        ln_a_b=row(ln_a_b), conv_b_w=conv_b_w[i], pool_bd=pool_bd.astype(_BF16), pool_scale=row(pool_scale),
        w_out=w_out[i].astype(_BF16), ln1_g=row(ln1_g), ln1_b=row(ln1_b),
        router_hi=router_hi, router_lo=router_lo, router_b=router_b[i][:, None],
        w_gate_up=w_gate_up[i].astype(_BF16), b_gate_up=b_gate_up[i][:, None, :],
        w_down=w_down[i].astype(_BF16), b_down=b_down[i][:, None, :],
        ple_w_gate=ple_w_gate[i].astype(_BF16), ple_w_proj=ple_w_proj[i].astype(_BF16),
        ln2_g=row(ln2_g), ln2_b=row(ln2_b),
    )


def kernel(x, p, w_in, conv_a_w, conv_a_b, ln_a_g, ln_a_b, conv_b_w, pool_w, pool_scale, w_out, ln1_g, ln1_b, router_w, router_b, w_gate_up, b_gate_up, w_down, b_down, ple_w_gate, ple_w_proj, ln2_g, ln2_b):
    depth = w_in.shape[0]
    bsz, seq, d = x.shape
    n = bsz * seq
    assert seq % SEQ_TILE == 0 and d % LANES == 0
    assert w_gate_up.shape[1] == N_EXPERTS and w_down.shape[2] % FFN_CHUNK == 0
    alpha = float((2 * depth) ** 0.25)
    n_tiles = n // SEQ_TILE
    n_blocks = -(-(n * TOP_K) // ROW_BLOCK) + N_EXPERTS
    n_rows = n_blocks * ROW_BLOCK

    for i in range(depth):
        lw = _layer_weights(i, w_in, conv_a_w, conv_a_b, ln_a_g, ln_a_b, conv_b_w, pool_w, pool_scale, w_out,
                            ln1_g, ln1_b, router_w, router_b, w_gate_up, b_gate_up, w_down, b_down,
                            ple_w_gate, ple_w_proj, ln2_g, ln2_b)
        h, idx, gates, pos, cnt = _mixer(x, lw, alpha)

        counts = cnt[:, 0].astype(jnp.int32)
        padded = (counts + ROW_BLOCK - 1) // ROW_BLOCK * ROW_BLOCK
        ends = jnp.cumsum(padded)
        starts = (ends - padded).astype(jnp.int32)
        block_expert = jnp.minimum(
            jnp.searchsorted(ends, jnp.arange(n_blocks, dtype=jnp.int32) * ROW_BLOCK, side="right"),
            N_EXPERTS - 1).astype(jnp.int32)

        idx_s = idx.reshape(n_tiles, 1, TOP_K * SEQ_TILE)
        pos_s = pos.reshape(n_tiles, 1, TOP_K * SEQ_TILE)
        gates_p = jnp.pad(gates, ((0, 0), (0, LANES - TOP_K), (0, 0)))
        h2d = h.reshape(n, d)

        buf = _dispatch(h2d, idx_s, pos_s, starts, n_rows)
        y = _experts(buf, block_expert, lw)
        x = _combine(h2d, p[i].reshape(n, -1), y, idx_s, pos_s, gates_p, starts, lw, alpha).reshape(bsz, seq, d)
    return x
```

```python
import functools

import jax
import jax.numpy as jnp
from jax import lax
from jax.experimental import pallas as pl
from jax.experimental.pallas import tpu as pltpu

CONF_KERNEL = 31
SHORT_KERNEL = 3
POOL_WINDOWS = (2, 4, 8, 16)
N_EXPERTS = 32
TOP_K = 4
SWIGLU_LIMIT = 7.0
SWIGLU_ALPHA = 1.702
LN_EPS = 1e-5

SUBLANES = 8
LANES = 128
VMEM_LIMIT_BYTES = 58 * 1024 * 1024

SEQ_TILE = 512
GROUP = SUBLANES
ROW_BLOCK = 512
GROUPS_PER_BLOCK = ROW_BLOCK // GROUP
SPARE_GROUPS = 2 * GROUPS_PER_BLOCK
PERM_CHUNK = 256
LOCAL_ROWS = -(-(TOP_K * SEQ_TILE + N_EXPERTS * (GROUP - 1)) // PERM_CHUNK) * PERM_CHUNK
LOCAL_GROUPS = LOCAL_ROWS // GROUP
CONV_CHUNK = 32
HALO = 32
FFN_CHUNK = 512

_BF16 = jnp.bfloat16
_F32 = jnp.float32
_U32 = jnp.uint32
_HIGH_HALF = 0xFFFF0000


def _layer_norm(v, g, b):
    mu = jnp.mean(v, axis=-1, keepdims=True)
    c = v - mu
    var = jnp.mean(c * c, axis=-1, keepdims=True)
    return c * lax.rsqrt(var + LN_EPS) * g + b


def _dot(a, b):
    return jnp.dot(a, b, preferred_element_type=_F32)


def _dot_nt(a, b):
    return lax.dot_general(a, b, (((1,), (1,)), ((), ())), preferred_element_type=_F32)


def _pack_pair(left, right):
    return pltpu.bitcast(left, _U32) | (pltpu.bitcast(right, _U32) >> 16)


def _unpack_pair(words):
    left = pltpu.bitcast(words & _U32(_HIGH_HALF), _F32)
    right = pltpu.bitcast(words << 16, _F32)
    return left.astype(_BF16), right.astype(_BF16)


def _mixer_kernel(alpha, d_conf, d_pool, n_st,
                  x_ref, w_in_ref, caw_ref, cab_ref, lag_ref, lab_ref, cbw_ref, pbd_ref, psc_ref,
                  w_out_ref, l1g_ref, l1b_ref, rwh_ref, rwl_ref, rb_ref,
                  h_ref, route_ref, cnt_ref, buf_ref,
                  ush_ref, cbx_ref, vx_ref, s2_ref, s4_ref, s8_ref, ycat_ref, tri_ref, hb_ref, loc_ref):
    ts, d = x_ref.shape
    half = d // 2
    t = pl.program_id(0)
    s = lax.rem(t, n_st)

    @pl.when(t == 0)
    def _():
        r = lax.broadcasted_iota(jnp.int32, (ts, ts), 0)
        c = lax.broadcasted_iota(jnp.int32, (ts, ts), 1)
        tri_ref[...] = (r < c).astype(_BF16)
        hb_ref[...] = jnp.zeros_like(hb_ref)
        loc_ref[...] = jnp.zeros_like(loc_ref)

    @pl.when(s == 0)
    def _():
        ush_ref[0, 0:HALO, :] = jnp.zeros((HALO, d_conf), _F32)
        cbx_ref[0:SUBLANES, :] = jnp.zeros((SUBLANES, d_conf), _F32)
        vx_ref[0:HALO, :] = jnp.zeros((HALO, d_pool), _F32)

    prev = [loc_ref[k:k + 1, :] for k in range(TOP_K)]
    rows = lax.broadcasted_iota(jnp.int32, (LOCAL_ROWS, ts), 0)
    hit = jnp.logical_or(jnp.logical_or(rows == prev[0], rows == prev[1]),
                         jnp.logical_or(rows == prev[2], rows == prev[3]))
    picked = _dot(hit.astype(_F32).astype(_BF16), hb_ref[...])
    buf_ref[...] = _pack_pair(picked[:, 0:half], picked[:, half:d])

    x = x_ref[...]
    xb = x.astype(_BF16)
    c1, c2, c3, c4, c5 = (d_conf * i for i in range(1, 6))

    za = _dot(xb, w_in_ref[:, 0:c2])
    u = za[:, 0:c1] * jax.nn.sigmoid(za[:, c1:c2])
    ush_ref[0, HALO:HALO + ts, :] = u
    for r in range(1, SUBLANES):
        ush_ref[r, 0:ts + HALO - SUBLANES, :] = ush_ref[0, r:r + ts + HALO - SUBLANES, :]

    for ci in range(ts // CONV_CHUNK):
        base = ci * CONV_CHUNK
        acc = jnp.zeros((CONV_CHUNK, d_conf), _F32)
        for j in range(CONF_KERNEL):
            q, r = divmod(HALO - (CONF_KERNEL - 1) + j, SUBLANES)
            acc = acc + caw_ref[j:j + 1, :] * ush_ref[r, base + q * SUBLANES:base + q * SUBLANES + CONV_CHUNK, :]
        acc = acc + cab_ref[...]
        ya = _layer_norm(acc, lag_ref[...], lab_ref[...])
        ya = ya * jax.nn.sigmoid(ya)
        ycat_ref[base:base + CONV_CHUNK, 0:d_conf] = ya.astype(_BF16)
    ush_ref[0, 0:HALO, :] = ush_ref[0, ts:ts + HALO, :]

    zb = _dot(xb, w_in_ref[:, c2:c5])
    cb = zb[:, c1:c2] * zb[:, c2:c3]
    cbx_ref[SUBLANES:SUBLANES + ts, :] = cb
    conv_b = cbw_ref[SHORT_KERNEL - 1:SHORT_KERNEL, :] * cb
    for j in range(SHORT_KERNEL - 1):
        off = SUBLANES - (SHORT_KERNEL - 1) + j
        conv_b = conv_b + cbw_ref[j:j + 1, :] * cbx_ref[off:off + ts, :]
    ycat_ref[:, c1:c2] = (zb[:, 0:c1] * conv_b).astype(_BF16)
    cbx_ref[0:SUBLANES, :] = cbx_ref[ts:ts + SUBLANES, :]

    v = _dot(xb, w_in_ref[:, c5:c5 + d_pool])
    vx_ref[HALO:HALO + ts, :] = v
    n8 = ts + HALO - 8
    s2_ref[8:8 + n8, :] = vx_ref[8:8 + n8, :] + vx_ref[7:7 + n8, :]
    n16 = ts + HALO - 16
    s4_ref[16:16 + n16, :] = s2_ref[16:16 + n16, :] + s2_ref[14:14 + n16, :]
    n24 = ts + HALO - 24
    s8_ref[24:24 + n24, :] = s4_ref[24:24 + n24, :] + s4_ref[20:20 + n24, :]
    s16 = s8_ref[HALO:HALO + ts, :] + s8_ref[HALO - 8:HALO - 8 + ts, :]
    sums = (s2_ref[HALO:HALO + ts, :], s4_ref[HALO:HALO + ts, :], s8_ref[HALO:HALO + ts, :], s16)
    tpos = s * ts + lax.broadcasted_iota(jnp.int32, (ts, 1), 0) + 1
    lane = lax.broadcasted_iota(jnp.int32, (ts, d_pool), 1)
    group = d_pool // len(POOL_WINDOWS)
    mean = None
    for gi, w in enumerate(POOL_WINDOWS):
        m = sums[gi] / jnp.minimum(tpos, w).astype(_F32)
        mean = m if mean is None else jnp.where(lane >= gi * group, m, mean)
    pooled = (mean - v).astype(_BF16)
    ycat_ref[:, c2:c2 + d_pool] = (_dot(pooled, pbd_ref[...]) * psc_ref[...]).astype(_BF16)
    vx_ref[0:HALO, :] = vx_ref[ts:ts + HALO, :]

    mix = _dot(ycat_ref[...], w_out_ref[...])
    h = _layer_norm(alpha * x + mix, l1g_ref[...], l1b_ref[...])
    h_ref[...] = h

    hh = h.astype(_BF16)
    hb_ref[...] = hh
    hl = (h - hh.astype(_F32)).astype(_BF16)
    logits = _dot_nt(rwh_ref[...], hh) + _dot_nt(rwh_ref[...], hl) + _dot_nt(rwl_ref[...], hh) + rb_ref[...]
    ne = logits.shape[0]
    eid = lax.broadcasted_iota(jnp.int32, (ne, ts), 0)
    work = logits
    sels, tops = [], []
    for k in range(TOP_K):
        m = jnp.max(work, axis=0, keepdims=True)
        ik = jnp.min(jnp.where(work == m, eid, ne), axis=0, keepdims=True)
        sel = eid == ik
        work = jnp.where(sel, -jnp.inf, work)
        sels.append(sel)
        tops.append(m)
    exps = [jnp.exp(t - tops[0]) for t in tops]
    denom = exps[0] + exps[1] + exps[2] + exps[3]
    for k in range(TOP_K):
        route_ref[k:k + 1, :] = exps[k] / denom

    chosen = jnp.logical_or(jnp.logical_or(sels[0], sels[1]), jnp.logical_or(sels[2], sels[3]))
    chosen_b = chosen.astype(_F32).astype(_BF16)
    before = _dot(chosen_b, tri_ref[...])
    count = jnp.sum(chosen.astype(_F32), axis=1, keepdims=True)
    padded = jnp.ceil(count * (1.0 / GROUP)) * GROUP
    lower = (lax.broadcasted_iota(jnp.int32, (ne, ne), 1) < lax.broadcasted_iota(jnp.int32, (ne, ne), 0))
    start = _dot(lower.astype(_F32).astype(_BF16), jnp.broadcast_to(padded, (ne, LANES)).astype(_BF16))[:, 0:1]
    where_to = before + start
    for k in range(TOP_K):
        lk = jnp.sum(jnp.where(sels[k], where_to, 0.0), axis=0, keepdims=True)
        route_ref[TOP_K + k:TOP_K + k + 1, :] = lk
        loc_ref[k:k + 1, :] = lk.astype(jnp.int32)
    cnt_ref[...] = _dot_nt(jnp.ones((SUBLANES, ts), _BF16), chosen_b)


def _mixer(x, lw, alpha):
    bsz, seq, d = x.shape
    ts = SEQ_TILE
    n_st = seq // ts
    d_conf = lw["conv_a_w"].shape[1]
    d_pool = lw["pool_scale"].shape[1]
    d_in = lw["w_in"].shape[1]
    const = lambda *shape: pl.BlockSpec(shape, lambda t: (0,) * len(shape))
    n_tiles = bsz * n_st
    tile_idx = lambda t: (t, 0, 0)
    x_idx = lambda t: (jnp.minimum(t, n_tiles - 1) // n_st, jnp.minimum(t, n_tiles - 1) % n_st, 0)
    kern = functools.partial(_mixer_kernel, alpha, d_conf, d_pool, n_st)
    return pl.pallas_call(
        kern,
        grid=(n_tiles + 1,),
        in_specs=[
            pl.BlockSpec((None, ts, d), x_idx),
            const(d, d_in), const(CONF_KERNEL, d_conf), const(1, d_conf), const(1, d_conf), const(1, d_conf),
            const(SHORT_KERNEL, d_conf), const(d_pool, d_pool), const(1, d_pool),
            const(d, d), const(1, d), const(1, d),
            const(N_EXPERTS, d), const(N_EXPERTS, d), const(N_EXPERTS, 1),
        ],
        out_specs=[
            pl.BlockSpec((ts, d), lambda t: (t, 0)),
            pl.BlockSpec((None, 2 * TOP_K, ts), tile_idx),
            pl.BlockSpec((None, SUBLANES, N_EXPERTS), tile_idx),
            pl.BlockSpec((None, LOCAL_ROWS, d // 2), lambda t: (jnp.maximum(t - 1, 0), 0, 0)),
        ],
        out_shape=[
            jax.ShapeDtypeStruct(((n_tiles + 1) * ts, d), _F32),
            jax.ShapeDtypeStruct((n_tiles + 1, 2 * TOP_K, ts), _F32),
            jax.ShapeDtypeStruct((n_tiles + 1, SUBLANES, N_EXPERTS), _F32),
            jax.ShapeDtypeStruct((n_tiles, LOCAL_ROWS, d // 2), _U32),
        ],
        scratch_shapes=[
            pltpu.VMEM((SUBLANES, ts + HALO, d_conf), _F32),
            pltpu.VMEM((ts + SUBLANES, d_conf), _F32),
            pltpu.VMEM((ts + HALO, d_pool), _F32),
            pltpu.VMEM((ts + HALO, d_pool), _F32),
            pltpu.VMEM((ts + HALO, d_pool), _F32),
            pltpu.VMEM((ts + HALO, d_pool), _F32),
            pltpu.VMEM((ts, d), _BF16),
            pltpu.VMEM((ts, ts), _BF16),
            pltpu.VMEM((ts, d), _BF16),
            pltpu.VMEM((SUBLANES, ts), jnp.int32),
        ],
        compiler_params=pltpu.CompilerParams(
            dimension_semantics=("arbitrary",), vmem_limit_bytes=VMEM_LIMIT_BYTES),
        name="mixer",
    )(x, lw["w_in"], lw["conv_a_w"], lw["conv_a_b"], lw["ln_a_g"], lw["ln_a_b"], lw["conv_b_w"],
      lw["pool_bd"], lw["pool_scale"], lw["w_out"], lw["ln1_g"], lw["ln1_b"],
      lw["router_hi"], lw["router_lo"], lw["router_b"])


def _group_copy(src_ref, src_group, dst_ref, dst_group, sem):
    return pltpu.make_async_copy(
        src_ref.at[pl.ds(pl.multiple_of(src_group * GROUP, GROUP), GROUP), :],
        dst_ref.at[pl.ds(pl.multiple_of(dst_group * GROUP, GROUP), GROUP), :], sem)


def _expert_kernel(be_ref, nv_ref, ug_ref, prev_ref, cur_ref, nxt_ref, buf_ref, wgu32_ref, bgu_ref, wd32_ref, bd_ref,
                   y_ref, xg_ref, yg_ref, wgu_ref, wd_ref, zero_ref, sem_in, sem_out, sem_zero):
    i = pl.program_id(0)
    n_valid = nv_ref[0]
    n_tiles = ug_ref.shape[0]
    slot = lax.rem(i, 2)
    other = 1 - slot
    half = xg_ref.shape[2]
    d_e = wd_ref.shape[0]

    def zero_fill(first_group, end_group):
        def copy(g):
            return pltpu.make_async_copy(
                zero_ref, y_ref.at[pl.ds(pl.multiple_of(g * GROUP, GROUP), GROUP), :], sem_zero)

        def start(g, c):
            copy(g).start()
            return c

        def wait(g, c):
            copy(g).wait()
            return c

        lax.fori_loop(first_group, end_group, start, 0)
        lax.fori_loop(first_group, end_group, wait, 0)

    @pl.when(i == 0)
    def _():
        zero_ref[...] = jnp.zeros_like(zero_ref)
        zero_fill(n_tiles * LOCAL_GROUPS, n_tiles * LOCAL_GROUPS + SPARE_GROUPS)

    @pl.when(i < n_tiles)
    def _():
        zero_fill(i * LOCAL_GROUPS + ug_ref[jnp.minimum(i, n_tiles - 1)], (i + 1) * LOCAL_GROUPS)

    @pl.when(jnp.logical_or(i == 0, be_ref[i] != be_ref[jnp.maximum(i - 1, 0)]))
    def _():
        wgu_ref[...] = wgu32_ref[...].astype(_BF16)
        wd_ref[...] = wd32_ref[...].astype(_BF16)

    def gather(tab_ref, to_slot):
        return [_group_copy(buf_ref, tab_ref[0, j], xg_ref.at[to_slot], j, sem_in.at[to_slot])
                for j in range(GROUPS_PER_BLOCK)]

    def scatter(tab_ref, from_slot):
        return [_group_copy(yg_ref.at[from_slot], j, y_ref, tab_ref[1, j], sem_out.at[from_slot])
                for j in range(GROUPS_PER_BLOCK)]

    @pl.when(jnp.logical_and(i == 0, n_valid > 0))
    def _():
        for c in gather(cur_ref, 0):
            c.start()

    @pl.when(i + 1 < n_valid)
    def _():
        for c in gather(nxt_ref, other):
            c.start()

    @pl.when(i < n_valid)
    def _():
        for c in gather(cur_ref, slot):
            c.wait()
        xl, xr = _unpack_pair(xg_ref[slot])
        acc = jnp.zeros((ROW_BLOCK, 2 * half), _F32)
        for j in range(d_e // FFN_CHUNK):
            lo, hi = j * FFN_CHUNK, (j + 1) * FFN_CHUNK
            gate = _dot(xl, wgu_ref[0:half, lo:hi]) + _dot(xr, wgu_ref[half:2 * half, lo:hi]) + bgu_ref[:, lo:hi]
            up = (_dot(xl, wgu_ref[0:half, d_e + lo:d_e + hi]) + _dot(xr, wgu_ref[half:2 * half, d_e + lo:d_e + hi])
                  + bgu_ref[:, d_e + lo:d_e + hi])
            gate = jnp.minimum(gate, SWIGLU_LIMIT)
            up = jnp.clip(up, -SWIGLU_LIMIT, SWIGLU_LIMIT)
            act = (up + 1.0) * gate * jax.nn.sigmoid(SWIGLU_ALPHA * gate)
            acc = acc + _dot(act.astype(_BF16), wd_ref[lo:hi, :])
        yb = (acc + bd_ref[...]).astype(_BF16).astype(_F32)
        yg_ref[slot] = _pack_pair(yb[:, 0:half], yb[:, half:2 * half])
        for c in scatter(cur_ref, slot):
            c.start()

    @pl.when(jnp.logical_and(i >= 1, i <= n_valid))
    def _():
        for c in scatter(prev_ref, other):
            c.wait()

    @pl.when(jnp.logical_and(i == pl.num_programs(0) - 1, n_valid == pl.num_programs(0)))
    def _():
        for c in scatter(cur_ref, slot):
            c.wait()


def _experts(buf2d, tables, block_expert, n_valid, used_groups, lw):
    n_blocks = tables.shape[0]
    half = buf2d.shape[1]
    d = 2 * half
    d_e = lw["w_down"].shape[1]
    y_rows = buf2d.shape[0] + SPARE_GROUPS * GROUP
    by_expert = lambda *shape: pl.BlockSpec((None,) + shape, lambda i, be, nv, ug: (be[i],) + (0,) * len(shape))
    table = lambda f: pl.BlockSpec((None, 2, GROUPS_PER_BLOCK), lambda i, be, nv, ug: (f(i), 0, 0),
                                   memory_space=pltpu.SMEM)
    return pl.pallas_call(
        _expert_kernel,
        grid_spec=pltpu.PrefetchScalarGridSpec(
            num_scalar_prefetch=3,
            grid=(n_blocks,),
            in_specs=[table(lambda i: jnp.maximum(i - 1, 0)), table(lambda i: i),
                      table(lambda i: jnp.minimum(i + 1, n_blocks - 1)),
                      pl.BlockSpec(memory_space=pl.ANY),
                      by_expert(d, 2 * d_e), by_expert(1, 2 * d_e), by_expert(d_e, d), by_expert(1, d)],
            out_specs=pl.BlockSpec(memory_space=pl.ANY),
            scratch_shapes=[pltpu.VMEM((2, ROW_BLOCK, half), _U32), pltpu.VMEM((2, ROW_BLOCK, half), _U32),
                            pltpu.VMEM((d, 2 * d_e), _BF16), pltpu.VMEM((d_e, d), _BF16),
                            pltpu.VMEM((GROUP, half), _U32),
                            pltpu.SemaphoreType.DMA((2,)), pltpu.SemaphoreType.DMA((2,)),
                            pltpu.SemaphoreType.DMA],
        ),
        out_shape=jax.ShapeDtypeStruct((y_rows, half), _U32),
        compiler_params=pltpu.CompilerParams(
            dimension_semantics=("arbitrary",), vmem_limit_bytes=VMEM_LIMIT_BYTES),
        name="experts",
    )(block_expert, n_valid, used_groups, tables, tables, tables, buf2d,
      lw["w_gate_up"], lw["b_gate_up"], lw["w_down"], lw["b_down"])


def _combine_kernel(alpha, route_ref, h_ref, p_ref, y_ref, pgw_ref, ppw_ref, l2g_ref, l2b_ref, out_ref):
    ts, d = h_ref.shape
    half = d // 2
    h = h_ref[...]
    ple = jax.nn.sigmoid(_dot(h.astype(_BF16), pgw_ref[...])) * _dot(p_ref[...].astype(_BF16), ppw_ref[...])

    route = jnp.concatenate([route_ref[...], jnp.zeros((LANES - 2 * TOP_K, ts), _F32)], axis=0)
    cols = jnp.transpose(route)
    left = jnp.zeros((ts, half), _F32)
    right = jnp.zeros((ts, half), _F32)
    for c in range(LOCAL_ROWS // PERM_CHUNK):
        r0 = c * PERM_CHUNK
        pos = (r0 + lax.broadcasted_iota(jnp.int32, (ts, PERM_CHUNK), 1)).astype(_F32)
        w = jnp.zeros((ts, PERM_CHUNK), _F32)
        for k in range(TOP_K):
            w = w + jnp.where(pos == cols[:, TOP_K + k:TOP_K + k + 1], cols[:, k:k + 1], 0.0)
        yl, yr = _unpack_pair(y_ref[r0:r0 + PERM_CHUNK, :])
        wb = w.astype(_BF16)
        left = left + _dot(wb, yl)
        right = right + _dot(wb, yr)
    ffn = jnp.concatenate([left, right], axis=1)
    out_ref[...] = _layer_norm(alpha * h + ffn + ple, l2g_ref[...], l2b_ref[...])


def _combine(h2d, p2d, y2d, route, lw, alpha):
    d = h2d.shape[1]
    ts = SEQ_TILE
    n, dp = p2d.shape
    const = lambda *shape: pl.BlockSpec(shape, lambda t: (0,) * len(shape))
    return pl.pallas_call(
        functools.partial(_combine_kernel, alpha),
        grid=(n // ts,),
        in_specs=[pl.BlockSpec((None, 2 * TOP_K, ts), lambda t: (t, 0, 0)),
                  pl.BlockSpec((ts, d), lambda t: (t, 0)),
                  pl.BlockSpec((ts, dp), lambda t: (t, 0)),
                  pl.BlockSpec((LOCAL_ROWS, d // 2), lambda t: (t, 0)),
                  const(d, d), const(dp, d), const(1, d), const(1, d)],
        out_specs=pl.BlockSpec((ts, d), lambda t: (t, 0)),
        out_shape=jax.ShapeDtypeStruct((n, d), _F32),
        compiler_params=pltpu.CompilerParams(
            dimension_semantics=("arbitrary",), vmem_limit_bytes=VMEM_LIMIT_BYTES),
        name="combine",
    )(route, h2d, p2d, y2d, lw["ple_w_gate"], lw["ple_w_proj"], lw["ln2_g"], lw["ln2_b"])


def _layer_weights(i, w_in, conv_a_w, conv_a_b, ln_a_g, ln_a_b, conv_b_w, pool_w, pool_scale, w_out,
                   ln1_g, ln1_b, router_w, router_b, w_gate_up, b_gate_up, w_down, b_down,
                   ple_w_gate, ple_w_proj, ln2_g, ln2_b):
    row = lambda a: a[i][None, :]
    n_groups, group = pool_w.shape[1], pool_w.shape[2]
    d_pool = n_groups * group
    pool_bd = jnp.zeros((d_pool, d_pool), _F32)
    for g in range(n_groups):
        pool_bd = lax.dynamic_update_slice(pool_bd, pool_w[i, g], (g * group, g * group))
    router_t = router_w[i].T
    router_hi = router_t.astype(_BF16)
    router_lo = (router_t - router_hi.astype(_F32)).astype(_BF16)
    return dict(
        w_in=w_in[i].astype(_BF16), conv_a_w=conv_a_w[i], conv_a_b=row(conv_a_b), ln_a_g=row(ln_a_g),
        ln_a_b=row(ln_a_b), conv_b_w=conv_b_w[i], pool_bd=pool_bd.astype(_BF16), pool_scale=row(pool_scale),
        w_out=w_out[i].astype(_BF16), ln1_g=row(ln1_g), ln1_b=row(ln1_b),
        router_hi=router_hi, router_lo=router_lo, router_b=router_b[i][:, None],
        w_gate_up=w_gate_up[i], b_gate_up=b_gate_up[i][:, None, :],
        w_down=w_down[i], b_down=b_down[i][:, None, :],
        ple_w_gate=ple_w_gate[i].astype(_BF16), ple_w_proj=ple_w_proj[i].astype(_BF16),
        ln2_g=row(ln2_g), ln2_b=row(ln2_b),
    )


def _block_tables(counts, n_blocks):
    n_tiles = counts.shape[0]
    groups = (counts + GROUP - 1) // GROUP
    local_start = jnp.cumsum(groups, axis=1) - groups
    used_groups = jnp.sum(groups, axis=1)
    upto = jnp.cumsum(groups, axis=0)
    total = upto[-1]
    region = (total + GROUPS_PER_BLOCK - 1) // GROUPS_PER_BLOCK * GROUPS_PER_BLOCK
    region_end = jnp.cumsum(region)
    region_start = region_end - region
    n_valid = (region_end[-1] // GROUPS_PER_BLOCK).astype(jnp.int32)

    q = jnp.arange(n_blocks * GROUPS_PER_BLOCK, dtype=jnp.int32)
    e_q = jnp.minimum(jnp.sum(q[:, None] >= region_end[None, :], axis=1), N_EXPERTS - 1).astype(jnp.int32)
    m = q - region_start[e_q]
    has_rows = jnp.logical_and(m < total[e_q], q < region_end[-1])
    upto_e = upto.T[e_q]
    tile = jnp.minimum(jnp.sum(upto_e <= m[:, None], axis=1), n_tiles - 1).astype(jnp.int32)
    before_tile = jnp.take_along_axis(upto_e, tile[:, None], axis=1)[:, 0] - groups[tile, e_q]
    src = tile * LOCAL_GROUPS + local_start[tile, e_q] + (m - before_tile)
    spare = n_tiles * LOCAL_GROUPS + q % SPARE_GROUPS
    gather_ids = jnp.where(has_rows, src, 0)
    scatter_ids = jnp.where(has_rows, src, spare)
    tables = jnp.stack([gather_ids, scatter_ids], axis=0).astype(jnp.int32)
    tables = tables.reshape(2, n_blocks, GROUPS_PER_BLOCK).transpose(1, 0, 2)
    block_expert = e_q.reshape(n_blocks, GROUPS_PER_BLOCK)[:, 0]
    return tables, block_expert, n_valid.reshape(1), used_groups.astype(jnp.int32)


def kernel(x, p, w_in, conv_a_w, conv_a_b, ln_a_g, ln_a_b, conv_b_w, pool_w, pool_scale, w_out, ln1_g, ln1_b, router_w, router_b, w_gate_up, b_gate_up, w_down, b_down, ple_w_gate, ple_w_proj, ln2_g, ln2_b):
    depth = w_in.shape[0]
    bsz, seq, d = x.shape
    n = bsz * seq
    assert seq % SEQ_TILE == 0 and d % (2 * LANES) == 0
    assert w_gate_up.shape[1] == N_EXPERTS and w_down.shape[2] % FFN_CHUNK == 0
    alpha = float((2 * depth) ** 0.25)
    n_tiles = n // SEQ_TILE
    max_groups = n_tiles * (TOP_K * SEQ_TILE // GROUP + N_EXPERTS) + N_EXPERTS * GROUPS_PER_BLOCK
    n_blocks = -(-max_groups // GROUPS_PER_BLOCK)

    for i in range(depth):
        lw = _layer_weights(i, w_in, conv_a_w, conv_a_b, ln_a_g, ln_a_b, conv_b_w, pool_w, pool_scale, w_out,
                            ln1_g, ln1_b, router_w, router_b, w_gate_up, b_gate_up, w_down, b_down,
                            ple_w_gate, ple_w_proj, ln2_g, ln2_b)
        h, route, cnt, buf = _mixer(x, lw, alpha)
        tables, block_expert, n_valid, used_groups = _block_tables(cnt[:n_tiles, 0, :].astype(jnp.int32), n_blocks)
        y = _experts(buf.reshape(n_tiles * LOCAL_ROWS, d // 2), tables, block_expert, n_valid, used_groups, lw)
        x = _combine(h, p[i].reshape(n, -1), y, route, lw, alpha).reshape(bsz, seq, d)
    return x
```

```python
import functools

import jax
import jax.numpy as jnp
from jax import lax
from jax.experimental import pallas as pl
from jax.experimental.pallas import tpu as pltpu

CONF_KERNEL = 31
SHORT_KERNEL = 3
POOL_WINDOWS = (2, 4, 8, 16)
N_EXPERTS = 32
TOP_K = 4
SWIGLU_LIMIT = 7.0
SWIGLU_ALPHA = 1.702
LN_EPS = 1e-5

SUBLANES = 8
LANES = 128
VMEM_LIMIT_BYTES = 58 * 1024 * 1024

SEQ_TILE = 512
GROUP = SUBLANES
ROW_BLOCK = 512
GROUPS_PER_BLOCK = ROW_BLOCK // GROUP
SPARE_GROUPS = 2 * GROUPS_PER_BLOCK
PERM_CHUNK = 256
LOCAL_ROWS = -(-(TOP_K * SEQ_TILE + N_EXPERTS * (GROUP - 1)) // PERM_CHUNK) * PERM_CHUNK
LOCAL_GROUPS = LOCAL_ROWS // GROUP
CONV_CHUNK = 32
HALO = 32
FFN_CHUNK = 512

_BF16 = jnp.bfloat16
_F32 = jnp.float32
_U32 = jnp.uint32
_HIGH_HALF = 0xFFFF0000


def _layer_norm(v, g, b):
    mu = jnp.mean(v, axis=-1, keepdims=True)
    c = v - mu
    var = jnp.mean(c * c, axis=-1, keepdims=True)
    return c * lax.rsqrt(var + LN_EPS) * g + b


def _dot(a, b):
    return jnp.dot(a, b, preferred_element_type=_F32)


def _dot_nt(a, b):
    return lax.dot_general(a, b, (((1,), (1,)), ((), ())), preferred_element_type=_F32)


def _pack_pair(left, right):
    return pltpu.bitcast(left, _U32) | (pltpu.bitcast(right, _U32) >> 16)


def _unpack_pair(words):
    left = pltpu.bitcast(words & _U32(_HIGH_HALF), _F32)
    right = pltpu.bitcast(words << 16, _F32)
    return left.astype(_BF16), right.astype(_BF16)


def _mixer_kernel(alpha, d_conf, d_pool, n_st,
                  x_ref, w_in_ref, caw_ref, cab_ref, lag_ref, lab_ref, cbw_ref, pbd_ref, psc_ref,
                  w_out_ref, l1g_ref, l1b_ref, rwh_ref, rwl_ref, rb_ref,
                  h_ref, route_ref, cnt_ref, buf_ref,
                  ush_ref, cbx_ref, vx_ref, s2_ref, s4_ref, s8_ref, ycat_ref, tri_ref, hprev_ref, oh_ref):
    ts, d = x_ref.shape
    half = d // 2
    t = pl.program_id(0)
    s = lax.rem(t, n_st)

    @pl.when(t == 0)
    def _():
        r = lax.broadcasted_iota(jnp.int32, (ts, ts), 0)
        c = lax.broadcasted_iota(jnp.int32, (ts, ts), 1)
        tri_ref[...] = (r < c).astype(_BF16)
        hprev_ref[...] = jnp.zeros_like(hprev_ref)

    @pl.when(s == 0)
    def _():
        ush_ref[0, 0:HALO, :] = jnp.zeros((HALO, d_conf), _F32)
        cbx_ref[0:SUBLANES, :] = jnp.zeros((SUBLANES, d_conf), _F32)
        vx_ref[0:HALO, :] = jnp.zeros((HALO, d_pool), _F32)

    hp = hprev_ref[...]

    hh = hp.astype(_BF16)
    hl = (hp - hh.astype(_F32)).astype(_BF16)
    logits = _dot_nt(rwh_ref[...], hh) + _dot_nt(rwh_ref[...], hl) + _dot_nt(rwl_ref[...], hh) + rb_ref[...]
    ne = logits.shape[0]
    eid = lax.broadcasted_iota(jnp.int32, (ne, ts), 0)
    work = logits
    sels, tops = [], []
    for k in range(TOP_K):
        m = jnp.max(work, axis=0, keepdims=True)
        ik = jnp.min(jnp.where(work == m, eid, ne), axis=0, keepdims=True)
        sel = eid == ik
        work = jnp.where(sel, -jnp.inf, work)
        sels.append(sel)
        tops.append(m)
    exps = [jnp.exp(tk - tops[0]) for tk in tops]
    denom = exps[0] + exps[1] + exps[2] + exps[3]
    for k in range(TOP_K):
        route_ref[k:k + 1, :] = exps[k] / denom

    chosen = jnp.logical_or(jnp.logical_or(sels[0], sels[1]), jnp.logical_or(sels[2], sels[3]))
    chosen_b = chosen.astype(_F32).astype(_BF16)
    before = _dot(chosen_b, tri_ref[...])
    count = jnp.sum(chosen.astype(_F32), axis=1, keepdims=True)
    padded = jnp.ceil(count * (1.0 / GROUP)) * GROUP
    lower = (lax.broadcasted_iota(jnp.int32, (ne, ne), 1) < lax.broadcasted_iota(jnp.int32, (ne, ne), 0))
    start = _dot(lower.astype(_F32).astype(_BF16), jnp.broadcast_to(padded, (ne, LANES)).astype(_BF16))[:, 0:1]
    where_to = before + start
    locs = []
    for k in range(TOP_K):
        lk = jnp.sum(jnp.where(sels[k], where_to, 0.0), axis=0, keepdims=True)
        route_ref[TOP_K + k:TOP_K + k + 1, :] = lk
        locs.append(lk.astype(jnp.int32))
    cnt_ref[...] = _dot_nt(jnp.ones((SUBLANES, ts), _BF16), chosen_b)

    def one_hot_rows(r0, n_rows):
        rows = r0 + lax.broadcasted_iota(jnp.int32, (n_rows, ts), 0)
        hit = jnp.logical_or(jnp.logical_or(rows == locs[0], rows == locs[1]),
                             jnp.logical_or(rows == locs[2], rows == locs[3]))
        oh_ref[r0:r0 + n_rows, :] = hit.astype(_F32).astype(_BF16)

    x = x_ref[...]
    xb = x.astype(_BF16)
    c1, c2, c3, c4, c5 = (d_conf * i for i in range(1, 6))
    third = LOCAL_ROWS // 3
    za = _dot(xb, w_in_ref[:, 0:c2])
    one_hot_rows(0, third)
    zb = _dot(xb, w_in_ref[:, c2:c4])
    one_hot_rows(third, third)
    zc = _dot(xb, w_in_ref[:, c4:c5 + d_pool])
    one_hot_rows(2 * third, third)

    u = za[:, 0:c1] * jax.nn.sigmoid(za[:, c1:c2])
    ush_ref[0, HALO:HALO + ts, :] = u
    for r in range(1, SUBLANES):
        ush_ref[r, 0:ts + HALO - SUBLANES, :] = ush_ref[0, r:r + ts + HALO - SUBLANES, :]

    quarter = half // 2
    n_chunks = ts // CONV_CHUNK
    picked_left = None
    for ci in range(n_chunks):
        base = ci * CONV_CHUNK
        acc = jnp.zeros((CONV_CHUNK, d_conf), _F32)
        for j in range(CONF_KERNEL):
            q, r = divmod(HALO - (CONF_KERNEL - 1) + j, SUBLANES)
            acc = acc + caw_ref[j:j + 1, :] * ush_ref[r, base + q * SUBLANES:base + q * SUBLANES + CONV_CHUNK, :]
        acc = acc + cab_ref[...]
        ya = _layer_norm(acc, lag_ref[...], lab_ref[...])
        ya = ya * jax.nn.sigmoid(ya)
        ycat_ref[base:base + CONV_CHUNK, 0:d_conf] = ya.astype(_BF16)
        if (ci + 1) % (n_chunks // 4) == 0:
            part = (ci + 1) // (n_chunks // 4) - 1
            lo = (part // 2) * quarter
            if part % 2 == 0:
                picked_left = _dot(oh_ref[...], hh[:, lo:lo + quarter])
            else:
                picked_right = _dot(oh_ref[...], hh[:, half + lo:half + lo + quarter])
                buf_ref[:, lo:lo + quarter] = _pack_pair(picked_left, picked_right)
    ush_ref[0, 0:HALO, :] = ush_ref[0, ts:ts + HALO, :]

    cb = zb[:, c1:c2] * zc[:, 0:c1]
    cbx_ref[SUBLANES:SUBLANES + ts, :] = cb
    conv_b = cbw_ref[SHORT_KERNEL - 1:SHORT_KERNEL, :] * cb
    for j in range(SHORT_KERNEL - 1):
        off = SUBLANES - (SHORT_KERNEL - 1) + j
        conv_b = conv_b + cbw_ref[j:j + 1, :] * cbx_ref[off:off + ts, :]
    ycat_ref[:, c1:c2] = (zb[:, 0:c1] * conv_b).astype(_BF16)
    cbx_ref[0:SUBLANES, :] = cbx_ref[ts:ts + SUBLANES, :]

    v = zc[:, c1:c1 + d_pool]
    vx_ref[HALO:HALO + ts, :] = v
    n8 = ts + HALO - 8
    s2_ref[8:8 + n8, :] = vx_ref[8:8 + n8, :] + vx_ref[7:7 + n8, :]
    n16 = ts + HALO - 16
    s4_ref[16:16 + n16, :] = s2_ref[16:16 + n16, :] + s2_ref[14:14 + n16, :]
    n24 = ts + HALO - 24
    s8_ref[24:24 + n24, :] = s4_ref[24:24 + n24, :] + s4_ref[20:20 + n24, :]
    s16 = s8_ref[HALO:HALO + ts, :] + s8_ref[HALO - 8:HALO - 8 + ts, :]
    sums = (s2_ref[HALO:HALO + ts, :], s4_ref[HALO:HALO + ts, :], s8_ref[HALO:HALO + ts, :], s16)
    tpos = s * ts + lax.broadcasted_iota(jnp.int32, (ts, 1), 0) + 1
    lane = lax.broadcasted_iota(jnp.int32, (ts, d_pool), 1)
    group = d_pool // len(POOL_WINDOWS)
    mean = None
    for gi, w in enumerate(POOL_WINDOWS):
        m = sums[gi] / jnp.minimum(tpos, w).astype(_F32)
        mean = m if mean is None else jnp.where(lane >= gi * group, m, mean)
    pooled = (mean - v).astype(_BF16)
    ycat_ref[:, c2:c2 + d_pool] = (_dot(pooled, pbd_ref[...]) * psc_ref[...]).astype(_BF16)
    vx_ref[0:HALO, :] = vx_ref[ts:ts + HALO, :]

    mix = _dot(ycat_ref[...], w_out_ref[...])
    h = _layer_norm(alpha * x + mix, l1g_ref[...], l1b_ref[...])
    h_ref[...] = h
    hprev_ref[...] = h


def _mixer(x, lw, alpha):
    bsz, seq, d = x.shape
    ts = SEQ_TILE
    n_st = seq // ts
    d_conf = lw["conv_a_w"].shape[1]
    d_pool = lw["pool_scale"].shape[1]
    d_in = lw["w_in"].shape[1]
    const = lambda *shape: pl.BlockSpec(shape, lambda t: (0,) * len(shape))
    n_tiles = bsz * n_st
    tile_idx = lambda t: (jnp.maximum(t - 1, 0), 0, 0)
    x_idx = lambda t: (jnp.minimum(t, n_tiles - 1) // n_st, jnp.minimum(t, n_tiles - 1) % n_st, 0)
    kern = functools.partial(_mixer_kernel, alpha, d_conf, d_pool, n_st)
    return pl.pallas_call(
        kern,
        grid=(n_tiles + 1,),
        in_specs=[
            pl.BlockSpec((None, ts, d), x_idx),
            const(d, d_in), const(CONF_KERNEL, d_conf), const(1, d_conf), const(1, d_conf), const(1, d_conf),
            const(SHORT_KERNEL, d_conf), const(d_pool, d_pool), const(1, d_pool),
            const(d, d), const(1, d), const(1, d),
            const(N_EXPERTS, d), const(N_EXPERTS, d), const(N_EXPERTS, 1),
        ],
        out_specs=[
            pl.BlockSpec((ts, d), lambda t: (t, 0)),
            pl.BlockSpec((None, 2 * TOP_K, ts), tile_idx),
            pl.BlockSpec((None, SUBLANES, N_EXPERTS), tile_idx),
            pl.BlockSpec((None, LOCAL_ROWS, d // 2), tile_idx),
        ],
        out_shape=[
            jax.ShapeDtypeStruct(((n_tiles + 1) * ts, d), _F32),
            jax.ShapeDtypeStruct((n_tiles, 2 * TOP_K, ts), _F32),
            jax.ShapeDtypeStruct((n_tiles, SUBLANES, N_EXPERTS), _F32),
            jax.ShapeDtypeStruct((n_tiles, LOCAL_ROWS, d // 2), _U32),
        ],
        scratch_shapes=[
            pltpu.VMEM((SUBLANES, ts + HALO, d_conf), _F32),
            pltpu.VMEM((ts + SUBLANES, d_conf), _F32),
            pltpu.VMEM((ts + HALO, d_pool), _F32),
            pltpu.VMEM((ts + HALO, d_pool), _F32),
            pltpu.VMEM((ts + HALO, d_pool), _F32),
            pltpu.VMEM((ts + HALO, d_pool), _F32),
            pltpu.VMEM((ts, d), _BF16),
            pltpu.VMEM((ts, ts), _BF16),
            pltpu.VMEM((ts, d), _F32),
            pltpu.VMEM((LOCAL_ROWS, ts), _BF16),
        ],
        compiler_params=pltpu.CompilerParams(
            dimension_semantics=("arbitrary",), vmem_limit_bytes=VMEM_LIMIT_BYTES),
        name="mixer",
    )(x, lw["w_in"], lw["conv_a_w"], lw["conv_a_b"], lw["ln_a_g"], lw["ln_a_b"], lw["conv_b_w"],
      lw["pool_bd"], lw["pool_scale"], lw["w_out"], lw["ln1_g"], lw["ln1_b"],
      lw["router_hi"], lw["router_lo"], lw["router_b"])


def _group_copy(src_ref, src_group, dst_ref, dst_group, sem):
    return pltpu.make_async_copy(
        src_ref.at[pl.ds(pl.multiple_of(src_group * GROUP, GROUP), GROUP), :],
        dst_ref.at[pl.ds(pl.multiple_of(dst_group * GROUP, GROUP), GROUP), :], sem)


def _expert_kernel(be_ref, nv_ref, ug_ref, prev_ref, cur_ref, nxt_ref, buf_ref, wgu32_ref, bgu_ref, wd32_ref, bd_ref,
                   y_ref, xg_ref, yg_ref, wgu_ref, wd_ref, zero_ref, sem_in, sem_out, sem_zero):
    i = pl.program_id(0)
    n_valid = nv_ref[0]
    n_tiles = ug_ref.shape[0]
    slot = lax.rem(i, 2)
    other = 1 - slot
    half = xg_ref.shape[2]
    d_e = wd_ref.shape[0]

    def zero_fill(first_group, end_group):
        def copy(g):
            return pltpu.make_async_copy(
                zero_ref, y_ref.at[pl.ds(pl.multiple_of(g * GROUP, GROUP), GROUP), :], sem_zero)

        def start(g, c):
            copy(g).start()
            return c

        def wait(g, c):
            copy(g).wait()
            return c

        lax.fori_loop(first_group, end_group, start, 0)
        lax.fori_loop(first_group, end_group, wait, 0)

    @pl.when(i == 0)
    def _():
        zero_ref[...] = jnp.zeros_like(zero_ref)
        zero_fill(n_tiles * LOCAL_GROUPS, n_tiles * LOCAL_GROUPS + SPARE_GROUPS)

    @pl.when(i < n_tiles)
    def _():
        zero_fill(i * LOCAL_GROUPS + ug_ref[jnp.minimum(i, n_tiles - 1)], (i + 1) * LOCAL_GROUPS)

    @pl.when(jnp.logical_or(i == 0, be_ref[i] != be_ref[jnp.maximum(i - 1, 0)]))
    def _():
        wgu_ref[...] = wgu32_ref[...].astype(_BF16)
        wd_ref[...] = wd32_ref[...].astype(_BF16)

    def gather(tab_ref, to_slot):
        return [_group_copy(buf_ref, tab_ref[0, j], xg_ref.at[to_slot], j, sem_in.at[to_slot])
                for j in range(GROUPS_PER_BLOCK)]

    def scatter(tab_ref, from_slot):
        return [_group_copy(yg_ref.at[from_slot], j, y_ref, tab_ref[1, j], sem_out.at[from_slot])
                for j in range(GROUPS_PER_BLOCK)]

    @pl.when(jnp.logical_and(i == 0, n_valid > 0))
    def _():
        for c in gather(cur_ref, 0):
            c.start()

    @pl.when(i + 1 < n_valid)
    def _():
        for c in gather(nxt_ref, other):
            c.start()

    @pl.when(i < n_valid)
    def _():
        for c in gather(cur_ref, slot):
            c.wait()
        xl, xr = _unpack_pair(xg_ref[slot])
        acc = jnp.zeros((ROW_BLOCK, 2 * half), _F32)
        for j in range(d_e // FFN_CHUNK):
            lo, hi = j * FFN_CHUNK, (j + 1) * FFN_CHUNK
            gate = _dot(xl, wgu_ref[0:half, lo:hi]) + _dot(xr, wgu_ref[half:2 * half, lo:hi]) + bgu_ref[:, lo:hi]
            up = (_dot(xl, wgu_ref[0:half, d_e + lo:d_e + hi]) + _dot(xr, wgu_ref[half:2 * half, d_e + lo:d_e + hi])
                  + bgu_ref[:, d_e + lo:d_e + hi])
            gate = jnp.minimum(gate, SWIGLU_LIMIT)
            up = jnp.clip(up, -SWIGLU_LIMIT, SWIGLU_LIMIT)
            act = (up + 1.0) * gate * jax.nn.sigmoid(SWIGLU_ALPHA * gate)
            acc = acc + _dot(act.astype(_BF16), wd_ref[lo:hi, :])
        yb = (acc + bd_ref[...]).astype(_BF16).astype(_F32)
        yg_ref[slot] = _pack_pair(yb[:, 0:half], yb[:, half:2 * half])
        for c in scatter(cur_ref, slot):
            c.start()

    @pl.when(jnp.logical_and(i >= 1, i <= n_valid))
    def _():
        for c in scatter(prev_ref, other):
            c.wait()

    @pl.when(jnp.logical_and(i == pl.num_programs(0) - 1, n_valid == pl.num_programs(0)))
    def _():
        for c in scatter(cur_ref, slot):
            c.wait()


def _experts(buf2d, tables, block_expert, n_valid, used_groups, layer, w_gate_up, b_gate_up, w_down, b_down):
    n_blocks = tables.shape[0]
    half = buf2d.shape[1]
    d = 2 * half
    d_e = w_down.shape[2]
    y_rows = buf2d.shape[0] + SPARE_GROUPS * GROUP
    by_expert = lambda *shape: pl.BlockSpec((None, None) + shape,
                                            lambda i, be, nv, ug: (layer, be[i]) + (0,) * len(shape))
    table = lambda f: pl.BlockSpec((None, 2, GROUPS_PER_BLOCK), lambda i, be, nv, ug: (f(i), 0, 0),
                                   memory_space=pltpu.SMEM)
    return pl.pallas_call(
        _expert_kernel,
        grid_spec=pltpu.PrefetchScalarGridSpec(
            num_scalar_prefetch=3,
            grid=(n_blocks,),
            in_specs=[table(lambda i: jnp.maximum(i - 1, 0)), table(lambda i: i),
                      table(lambda i: jnp.minimum(i + 1, n_blocks - 1)),
                      pl.BlockSpec(memory_space=pl.ANY),
                      by_expert(d, 2 * d_e), by_expert(1, 2 * d_e), by_expert(d_e, d), by_expert(1, d)],
            out_specs=pl.BlockSpec(memory_space=pl.ANY),
            scratch_shapes=[pltpu.VMEM((2, ROW_BLOCK, half), _U32), pltpu.VMEM((2, ROW_BLOCK, half), _U32),
                            pltpu.VMEM((d, 2 * d_e), _BF16), pltpu.VMEM((d_e, d), _BF16),
                            pltpu.VMEM((GROUP, half), _U32),
                            pltpu.SemaphoreType.DMA((2,)), pltpu.SemaphoreType.DMA((2,)),
                            pltpu.SemaphoreType.DMA],
        ),
        out_shape=jax.ShapeDtypeStruct((y_rows, half), _U32),
        compiler_params=pltpu.CompilerParams(
            dimension_semantics=("arbitrary",), vmem_limit_bytes=VMEM_LIMIT_BYTES),
        name="experts",
    )(block_expert, n_valid, used_groups, tables, tables, tables, buf2d, w_gate_up, b_gate_up, w_down, b_down)


def _combine_kernel(alpha, route_ref, h_ref, p_ref, y_ref, pgw_ref, ppw_ref, l2g_ref, l2b_ref, out_ref):
    ts, d = h_ref.shape
    half = d // 2
    h = h_ref[...]
    ple = jax.nn.sigmoid(_dot(h.astype(_BF16), pgw_ref[...])) * _dot(p_ref[...].astype(_BF16), ppw_ref[...])

    route = jnp.concatenate([route_ref[...], jnp.zeros((LANES - 2 * TOP_K, ts), _F32)], axis=0)
    cols = jnp.transpose(route)
    left = jnp.zeros((ts, half), _F32)
    right = jnp.zeros((ts, half), _F32)
    for c in range(LOCAL_ROWS // PERM_CHUNK):
        r0 = c * PERM_CHUNK
        pos = (r0 + lax.broadcasted_iota(jnp.int32, (ts, PERM_CHUNK), 1)).astype(_F32)
        w = jnp.zeros((ts, PERM_CHUNK), _F32)
        for k in range(TOP_K):
            w = w + jnp.where(pos == cols[:, TOP_K + k:TOP_K + k + 1], cols[:, k:k + 1], 0.0)
        yl, yr = _unpack_pair(y_ref[r0:r0 + PERM_CHUNK, :])
        wb = w.astype(_BF16)
        left = left + _dot(wb, yl)
        right = right + _dot(wb, yr)
    ffn = jnp.concatenate([left, right], axis=1)
    out_ref[...] = _layer_norm(alpha * h + ffn + ple, l2g_ref[...], l2b_ref[...])


def _combine(h2d, p3d, layer, y2d, route, lw, alpha):
    d = h2d.shape[1]
    ts = SEQ_TILE
    _, n, dp = p3d.shape
    const = lambda *shape: pl.BlockSpec(shape, lambda t: (0,) * len(shape))
    return pl.pallas_call(
        functools.partial(_combine_kernel, alpha),
        grid=(n // ts,),
        in_specs=[pl.BlockSpec((None, 2 * TOP_K, ts), lambda t: (t, 0, 0)),
                  pl.BlockSpec((ts, d), lambda t: (t, 0)),
                  pl.BlockSpec((None, ts, dp), lambda t: (layer, t, 0)),
                  pl.BlockSpec((LOCAL_ROWS, d // 2), lambda t: (t, 0)),
                  const(d, d), const(dp, d), const(1, d), const(1, d)],
        out_specs=pl.BlockSpec((ts, d), lambda t: (t, 0)),
        out_shape=jax.ShapeDtypeStruct((n, d), _F32),
        compiler_params=pltpu.CompilerParams(
            dimension_semantics=("arbitrary",), vmem_limit_bytes=VMEM_LIMIT_BYTES),
        name="combine",
    )(route, h2d, p3d, y2d, lw["ple_w_gate"], lw["ple_w_proj"], lw["ln2_g"], lw["ln2_b"])


def _layer_weights(i, w_in, conv_a_w, conv_a_b, ln_a_g, ln_a_b, conv_b_w, pool_w, pool_scale, w_out,
                   ln1_g, ln1_b, router_w, router_b, ple_w_gate, ple_w_proj, ln2_g, ln2_b):
    row = lambda a: a[i][None, :]
    n_groups, group = pool_w.shape[1], pool_w.shape[2]
    d_pool = n_groups * group
    pool_bd = jnp.zeros((d_pool, d_pool), _F32)
    for g in range(n_groups):
        pool_bd = lax.dynamic_update_slice(pool_bd, pool_w[i, g], (g * group, g * group))
    router_t = router_w[i].T
    router_hi = router_t.astype(_BF16)
    router_lo = (router_t - router_hi.astype(_F32)).astype(_BF16)
    return dict(
        w_in=w_in[i].astype(_BF16), conv_a_w=conv_a_w[i], conv_a_b=row(conv_a_b), ln_a_g=row(ln_a_g),
        ln_a_b=row(ln_a_b), conv_b_w=conv_b_w[i], pool_bd=pool_bd.astype(_BF16), pool_scale=row(pool_scale),
        w_out=w_out[i].astype(_BF16), ln1_g=row(ln1_g), ln1_b=row(ln1_b),
        router_hi=router_hi, router_lo=router_lo, router_b=router_b[i][:, None],
        ple_w_gate=ple_w_gate[i].astype(_BF16), ple_w_proj=ple_w_proj[i].astype(_BF16),
        ln2_g=row(ln2_g), ln2_b=row(ln2_b),
    )


def _block_tables(counts, n_blocks):
    n_tiles = counts.shape[0]
    groups = (counts + GROUP - 1) // GROUP
    local_start = jnp.cumsum(groups, axis=1) - groups
    used_groups = jnp.sum(groups, axis=1)
    upto = jnp.cumsum(groups, axis=0)
    total = upto[-1]
    region = (total + GROUPS_PER_BLOCK - 1) // GROUPS_PER_BLOCK * GROUPS_PER_BLOCK
    region_end = jnp.cumsum(region)
    region_start = region_end - region
    n_valid = (region_end[-1] // GROUPS_PER_BLOCK).astype(jnp.int32)

    f32 = lambda a: a.astype(_F32)
    pick = lambda one_hot, table: jnp.dot(one_hot, table, precision=lax.Precision.HIGHEST)
    q = jnp.arange(n_blocks * GROUPS_PER_BLOCK, dtype=jnp.int32)
    in_region = f32(jnp.logical_and(q[:, None] >= region_start[None, :], q[:, None] < region_end[None, :]))
    m = f32(q) - pick(in_region, f32(region_start))
    has_rows = jnp.logical_and(m < pick(in_region, f32(total)), q < region_end[-1])
    per_tile = pick(in_region, f32(jnp.concatenate([upto.T, groups.T, local_start.T], axis=1)))
    upto_e, groups_e, start_e = (per_tile[:, j * n_tiles:(j + 1) * n_tiles] for j in range(3))
    tile = jnp.minimum(jnp.sum(upto_e <= m[:, None], axis=1), n_tiles - 1)
    at_tile = f32(jnp.arange(n_tiles)[None, :] == tile[:, None])
    before_tile = jnp.sum(at_tile * (upto_e - groups_e), axis=1)
    src = (f32(tile) * LOCAL_GROUPS + jnp.sum(at_tile * start_e, axis=1) + (m - before_tile)).astype(jnp.int32)
    spare = n_tiles * LOCAL_GROUPS + q % SPARE_GROUPS
    gather_ids = jnp.where(has_rows, src, 0)
    scatter_ids = jnp.where(has_rows, src, spare)
    tables = jnp.stack([gather_ids, scatter_ids], axis=0).astype(jnp.int32)
    tables = tables.reshape(2, n_blocks, GROUPS_PER_BLOCK).transpose(1, 0, 2)
    first = in_region.reshape(n_blocks, GROUPS_PER_BLOCK, N_EXPERTS)[:, 0, :]
    block_expert = jnp.where(jnp.sum(first, axis=1) > 0, pick(first, f32(jnp.arange(N_EXPERTS))), N_EXPERTS - 1)
    return tables, block_expert.astype(jnp.int32), n_valid.reshape(1), used_groups.astype(jnp.int32)


def kernel(x, p, w_in, conv_a_w, conv_a_b, ln_a_g, ln_a_b, conv_b_w, pool_w, pool_scale, w_out, ln1_g, ln1_b, router_w, router_b, w_gate_up, b_gate_up, w_down, b_down, ple_w_gate, ple_w_proj, ln2_g, ln2_b):
    depth = w_in.shape[0]
    bsz, seq, d = x.shape
    n = bsz * seq
    assert seq % SEQ_TILE == 0 and d % (2 * LANES) == 0
    assert w_gate_up.shape[1] == N_EXPERTS and w_down.shape[2] % FFN_CHUNK == 0
    alpha = float((2 * depth) ** 0.25)
    n_tiles = n // SEQ_TILE
    max_groups = n_tiles * (TOP_K * SEQ_TILE // GROUP + N_EXPERTS) + N_EXPERTS * GROUPS_PER_BLOCK
    n_blocks = -(-max_groups // GROUPS_PER_BLOCK)

    expert_weights = (w_gate_up, b_gate_up[:, :, None, :], w_down, b_down[:, :, None, :])
    p3d = p.reshape(depth, n, -1)
    for i in range(depth):
        lw = _layer_weights(i, w_in, conv_a_w, conv_a_b, ln_a_g, ln_a_b, conv_b_w, pool_w, pool_scale, w_out,
                            ln1_g, ln1_b, router_w, router_b, ple_w_gate, ple_w_proj, ln2_g, ln2_b)
        h, route, cnt, buf = _mixer(x, lw, alpha)
        tables, block_expert, n_valid, used_groups = _block_tables(cnt[:, 0, :].astype(jnp.int32), n_blocks)
        y = _experts(buf.reshape(n_tiles * LOCAL_ROWS, d // 2), tables, block_expert, n_valid, used_groups,
                     i, *expert_weights)
        x = _combine(h, p3d, i, y, route, lw, alpha).reshape(bsz, seq, d)
    return x
```
